```python
import math
import jax, jax.numpy as jnp
from jax import lax
import numpy as np

D_MODEL = 1024
BATCH = 32
SEQ = 2048
DEPTH = 1

CHUNK = 64
Q_BLOCK = 128
HEAD_DIM = 64
N_HEADS_DSA = 8
N_HEADS_SB = 8
IDX_HEADS = 8
IDX_DIM = 64
TOPK_MAX = 256
N_REL_BUCKETS = 32
REL_MAX_DIST = 128
D_FF = 2816
CONV_WIDTH = 3
LN_EPS = 1e-5
DEEPNORM_ALPHA = (2.0 * DEPTH) ** 0.25
DEEPNORM_BETA = (8.0 * DEPTH) ** -0.25
WIDTH_DSA = N_HEADS_DSA * HEAD_DIM
WIDTH_SB = N_HEADS_SB * HEAD_DIM
IN_SIZES = (WIDTH_DSA, WIDTH_DSA, WIDTH_DSA, IDX_HEADS * IDX_DIM, IDX_DIM, IDX_HEADS,
            WIDTH_SB, WIDTH_SB, WIDTH_SB, D_MODEL, D_MODEL)
N_IN = sum(IN_SIZES)
IN_OFFSETS = tuple(int(o) for o in np.cumsum(IN_SIZES)[:-1])

kernel_name = 'hybrid_dsa_stickbreak_convffn_deepnorm'


def layer_norm(x, g, b):
    xf = x.astype(jnp.float32)
    mu = jnp.mean(xf, axis=-1, keepdims=True)
    var = jnp.mean(jnp.square(xf - mu), axis=-1, keepdims=True)
    return ((xf - mu) * lax.rsqrt(var + LN_EPS)).astype(x.dtype) * g + b


def t5_bucket(rel):
    nb = N_REL_BUCKETS // 2
    max_exact = nb // 2
    side = jnp.where(rel > 0, nb, 0)
    n = jnp.abs(rel)
    nf = jnp.maximum(n, 1).astype(jnp.float32)
    large = max_exact + (jnp.log(nf / max_exact) / math.log(REL_MAX_DIST / max_exact)
                         * (nb - max_exact)).astype(jnp.int32)
    large = jnp.minimum(large, nb - 1)
    return side + jnp.where(n < max_exact, n, large)


def dsa_sequence(q, k, v, q_idx, k_idx, w_idx, rel_bias):
    S = q.shape[0]
    n_sel = min(TOPK_MAX, S // 4)
    nb = S // Q_BLOCK
    key_chunk = jnp.arange(S) // CHUNK

    def block(args):
        qb, qib, wib, t0 = args
        t = t0 + jnp.arange(Q_BLOCK)
        q_chunk = (t // CHUNK)[:, None]
        dots = jnp.einsum('thd,sd->ths', qib, k_idx).astype(jnp.float32) * IDX_DIM ** -0.5
        score = jnp.einsum('th,ths->ts', wib.astype(jnp.float32), jax.nn.relu(dots))
        score = jnp.where(key_chunk[None, :] <= q_chunk, score, -jnp.inf)
        _, sel = lax.top_k(score, n_sel)
        valid = (sel // CHUNK) <= q_chunk
        kg = k[sel]
        vg = v[sel]
        logits = jnp.einsum('thd,tnhd->htn', qb, kg).astype(jnp.float32) * HEAD_DIM ** -0.5
        bias = rel_bias[t5_bucket(sel - t[:, None])]
        logits = logits + jnp.transpose(bias, (2, 0, 1)).astype(jnp.float32)
        logits = jnp.where(valid[None], logits, -jnp.inf)
        p = jax.nn.softmax(logits, axis=-1).astype(v.dtype)
        return jnp.einsum('htn,tnhd->thd', p, vg)

    blocks = (q.reshape(nb, Q_BLOCK, N_HEADS_DSA, HEAD_DIM),
              q_idx.reshape(nb, Q_BLOCK, IDX_HEADS, IDX_DIM),
              w_idx.reshape(nb, Q_BLOCK, IDX_HEADS),
              jnp.arange(nb) * Q_BLOCK)
    out = lax.map(block, blocks)
    return out.reshape(S, WIDTH_DSA)


def sb_sequence(q, k, v):
    S = q.shape[0]
    nb = S // Q_BLOCK
    s_pos = jnp.arange(S)

    def block(args):
        qb, t0 = args
        t = t0 + jnp.arange(Q_BLOCK)
        causal = s_pos[None, :] < t[:, None]
        z = jnp.einsum('thd,shd->hts', qb, k).astype(jnp.float32) * HEAD_DIM ** -0.5
        log_beta = jax.nn.log_sigmoid(z)
        log_keep = jnp.where(causal, jax.nn.log_sigmoid(-z), 0.0)
        later = lax.cumsum(log_keep, axis=2, reverse=True) - log_keep
        a = jnp.where(causal, jnp.exp(log_beta + later), 0.0).astype(v.dtype)
        return jnp.einsum('hts,shd->thd', a, v)

    out = lax.map(block, (q.reshape(nb, Q_BLOCK, N_HEADS_SB, HEAD_DIM), jnp.arange(nb) * Q_BLOCK))
    return out.reshape(S, WIDTH_SB)


def causal_dwconv(u, w, b):
    out = lax.conv_general_dilated(u, w[:, None, :], window_strides=(1,),
                                   padding=[(CONV_WIDTH - 1, 0)],
                                   dimension_numbers=('NWC', 'WIO', 'NWC'),
                                   feature_group_count=u.shape[-1])
    return out + b


def setup_inputs(seed: int = 0) -> dict:
    key = jax.random.key(seed)
    ks = jax.random.split(key, 16)
    f32 = jnp.float32
    nrm = lambda k, shape, scale: jax.random.normal(k, shape, f32) * scale
    return {
        'x': nrm(ks[0], (BATCH, SEQ, D_MODEL), 1.0),
        'rel_bias': nrm(ks[1], (N_REL_BUCKETS, N_HEADS_DSA), 0.5),
        'w_in': nrm(ks[2], (DEPTH, D_MODEL, N_IN), D_MODEL ** -0.5),
        'b_gates': nrm(ks[3], (DEPTH, 2, D_MODEL), 0.02),
        'w_branch_dsa': nrm(ks[4], (DEPTH, WIDTH_DSA, D_MODEL), WIDTH_DSA ** -0.5),
        'w_branch_sb': nrm(ks[5], (DEPTH, WIDTH_SB, D_MODEL), WIDTH_SB ** -0.5),
        'w_out': nrm(ks[6], (DEPTH, D_MODEL, D_MODEL), DEEPNORM_BETA * D_MODEL ** -0.5),
        'ln1_g': 1.0 + nrm(ks[7], (DEPTH, D_MODEL), 0.02),
        'ln1_b': nrm(ks[8], (DEPTH, D_MODEL), 0.02),
        'w_ffn_in': nrm(ks[9], (DEPTH, D_MODEL, 2 * D_FF), D_MODEL ** -0.5),
        'conv_w': nrm(ks[10], (DEPTH, CONV_WIDTH, D_FF), CONV_WIDTH ** -0.5),
        'conv_b': nrm(ks[11], (DEPTH, D_FF), 0.02),
        'w_ffn_out': nrm(ks[12], (DEPTH, D_FF, D_MODEL), DEEPNORM_BETA * D_FF ** -0.5),
        'ln2_g': 1.0 + nrm(ks[13], (DEPTH, D_MODEL), 0.02),
        'ln2_b': nrm(ks[14], (DEPTH, D_MODEL), 0.02),
    }


def reference(x, rel_bias, w_in, b_gates, w_branch_dsa, w_branch_sb, w_out, ln1_g, ln1_b,
              w_ffn_in, conv_w, conv_b, w_ffn_out, ln2_g, ln2_b):
    B, S, _ = x.shape
    for layer in range(DEPTH):
        proj = x @ w_in[layer]
        q_a, k_a, v_a, q_i, k_i, w_i, q_s, k_s, v_s, g_a, g_b = jnp.split(proj, IN_OFFSETS, axis=-1)
        w_i = w_i * IDX_HEADS ** -0.5
        y_a = lax.map(lambda a: dsa_sequence(*a, rel_bias),
                      (q_a.reshape(B, S, N_HEADS_DSA, HEAD_DIM),
                       k_a.reshape(B, S, N_HEADS_DSA, HEAD_DIM),
                       v_a.reshape(B, S, N_HEADS_DSA, HEAD_DIM),
                       q_i.reshape(B, S, IDX_HEADS, IDX_DIM), k_i, w_i))
        y_s = lax.map(lambda a: sb_sequence(*a),
                      (q_s.reshape(B, S, N_HEADS_SB, HEAD_DIM),
                       k_s.reshape(B, S, N_HEADS_SB, HEAD_DIM),
                       v_s.reshape(B, S, N_HEADS_SB, HEAD_DIM)))
        gate_a = jax.nn.sigmoid(g_a + b_gates[layer, 0])
        gate_b = jax.nn.sigmoid(g_b + b_gates[layer, 1])
        merged = gate_a * (y_a @ w_branch_dsa[layer]) + gate_b * (y_s @ w_branch_sb[layer])
        x = layer_norm(DEEPNORM_ALPHA * x + merged @ w_out[layer], ln1_g[layer], ln1_b[layer])
        u, g = jnp.split(x @ w_ffn_in[layer], 2, axis=-1)
        h = jax.nn.gelu(causal_dwconv(u, conv_w[layer], conv_b[layer])) * g
        x = layer_norm(DEEPNORM_ALPHA * x + h @ w_ffn_out[layer], ln2_g[layer], ln2_b[layer])
    return x
```

```python
import functools
import math

import jax
import jax.numpy as jnp
from jax import lax
from jax.experimental import pallas as pl
from jax.experimental.pallas import tpu as pltpu

F32 = jnp.float32
BF16 = jnp.bfloat16
I32 = jnp.int32

HEAD_DIM = 64
N_HEADS = 8
WIDTH = N_HEADS * HEAD_DIM
IDX_DIM = 64
CHUNK = 64
CHUNK_SHIFT = 6
TOPK_MAX = 256
N_REL_BUCKETS = 32
FAR_BUCKET = N_REL_BUCKETS // 2 - 1
CONV_WIDTH = 3
LN_EPS = 1e-5
TQ = 256
TK = 256
INT_MIN = -(2 ** 31)
NEG_BIG = -1e30
SB_UNDERFLOW = -104.0
VMEM_LIMIT = 56 * 1024 * 1024

NT_DIMS = (((1,), (1,)), ((), ()))
TN_DIMS = (((0,), (0,)), ((), ()))


def _params(n_axes, vmem=VMEM_LIMIT, **kw):
    return pltpu.CompilerParams(dimension_semantics=("arbitrary",) * n_axes,
                                vmem_limit_bytes=vmem, **kw)


def _proj_kernel(x_ref, wa_ref, wqi_ref, wki_ref, wwt_ref, ws_ref,
                 qkva_ref, qi_ref, ki_ref, wt_ref, qkvs_ref):
    xb = x_ref[...].astype(BF16)
    qkva_ref[...] = jnp.dot(xb, wa_ref[...], preferred_element_type=F32).astype(BF16)
    qi_ref[...] = jnp.dot(xb, wqi_ref[...], preferred_element_type=F32).astype(BF16)
    ki_ref[...] = jnp.dot(xb, wki_ref[...], preferred_element_type=F32).astype(BF16)
    wt_ref[0] = lax.dot_general(wwt_ref[...], xb, NT_DIMS, preferred_element_type=F32)
    qkvs_ref[...] = jnp.dot(xb, ws_ref[...], preferred_element_type=F32).astype(BF16)


def _proj(x2, wa, wqi, wki, wwt, ws, batch, seq, tm):
    m, d = x2.shape
    per_seq = seq // tm
    const = lambda i: (0, 0)
    row = lambda i: (i, 0)
    return pl.pallas_call(
        _proj_kernel,
        grid=(m // tm,),
        in_specs=[pl.BlockSpec((tm, d), row),
                  pl.BlockSpec(wa.shape, const), pl.BlockSpec(wqi.shape, const),
                  pl.BlockSpec(wki.shape, const), pl.BlockSpec(wwt.shape, const),
                  pl.BlockSpec(ws.shape, const)],
        out_specs=[pl.BlockSpec((tm, 3 * WIDTH), row), pl.BlockSpec((tm, WIDTH), row),
                   pl.BlockSpec((tm, IDX_DIM), row),
                   pl.BlockSpec((1, N_HEADS, tm), lambda i: (i // per_seq, 0, i % per_seq)),
                   pl.BlockSpec((tm, 3 * WIDTH), row)],
        out_shape=[jax.ShapeDtypeStruct((m, 3 * WIDTH), BF16),
                   jax.ShapeDtypeStruct((m, WIDTH), BF16),
                   jax.ShapeDtypeStruct((m, IDX_DIM), BF16),
                   jax.ShapeDtypeStruct((batch, N_HEADS, seq), F32),
                   jax.ShapeDtypeStruct((m, 3 * WIDTH), BF16)],
        compiler_params=_params(1),
        name="proj",
    )(x2, wa, wqi, wki, wwt, ws)


def _t5_bucket(rel):
    n = jnp.abs(rel)
    large = jnp.full(rel.shape, 8, I32)
    for edge in (12, 16, 23, 32, 46, 64, 91):
        large = large + jnp.where(n >= edge, 1, 0)
    return jnp.where(rel > 0, N_REL_BUCKETS // 2, 0) + jnp.where(n < 8, n, large)


def _relbias_kernel(relb_ref, out_ref):
    o = pl.program_id(0)
    h = pl.program_id(1)
    s = lax.broadcasted_iota(I32, (TK, TQ), 0)
    t = lax.broadcasted_iota(I32, (TK, TQ), 1)
    bucket = _t5_bucket((o - 1) * TK + s - t)
    acc = jnp.zeros((TK, TQ), F32)
    for b in range(N_REL_BUCKETS):
        acc = jnp.where(bucket == b, relb_ref[b, h], acc)
    out_ref[0, 0] = acc


def _relbias_tiles(rel_bias):
    return pl.pallas_call(
        _relbias_kernel,
        grid=(2, N_HEADS),
        in_specs=[pl.BlockSpec(memory_space=pltpu.SMEM)],
        out_specs=pl.BlockSpec((1, 1, TK, TQ), lambda o, h: (o, h, 0, 0)),
        out_shape=jax.ShapeDtypeStruct((2, N_HEADS, TK, TQ), F32),
        compiler_params=_params(2),
        name="relbias",
    )(rel_bias)


def _dsa_kernel(relb_ref, qa_ref, ka_ref, va_ref, qi_ref, ki_ref, wt_ref, nb_ref,
                out_ref, keys_ref, sel_ref, *, n_sel):
    j = pl.program_id(1)
    n_tiles = j + 1
    s_loc = lax.broadcasted_iota(I32, (TK, TQ), 0)
    t_loc = lax.broadcasted_iota(I32, (TK, TQ), 1)
    q_chunk = (t_loc + j * TQ) >> CHUNK_SHIFT

    def tile(kt):
        return pl.ds(pl.multiple_of(kt * TK, TK), TK)

    w_scaled = wt_ref[0] * (IDX_DIM ** -0.5 * N_HEADS ** -0.5)

    def score_tile(kt, carry):
        k_idx = ki_ref[tile(kt), :]
        acc = jnp.zeros((TK, TQ), F32)
        for h in range(N_HEADS):
            d = lax.dot_general(k_idx, qi_ref[:, h * IDX_DIM:(h + 1) * IDX_DIM], NT_DIMS,
                                preferred_element_type=F32)
            acc = acc + w_scaled[h:h + 1, :] * jnp.maximum(d, 0.0)
        bits = pltpu.bitcast(acc, I32)
        key = jnp.where(bits < 0, bits ^ 0x7FFFFFFF, bits)
        valid = ((s_loc + kt * TK) >> CHUNK_SHIFT) <= q_chunk
        keys_ref[tile(kt), :] = jnp.where(valid, key, INT_MIN)
        return carry

    lax.fori_loop(0, n_tiles, score_tile, 0)

    def count(pred_of_keys):
        def body(kt, c):
            return c + jnp.sum(jnp.where(pred_of_keys(keys_ref[tile(kt), :]), 1.0, 0.0),
                               axis=0, keepdims=True)
        return lax.fori_loop(0, n_tiles, body, jnp.zeros((1, TQ), F32))

    def bit_step(i, thr):
        cand = thr ^ lax.shift_left(jnp.int32(1), 31 - i)
        return jnp.where(count(lambda k: k >= cand) >= n_sel, cand, thr)

    thr = lax.fori_loop(0, 32, bit_step, jnp.full((1, TQ), INT_MIN, I32))

    need = n_sel - count(lambda k: k > thr)
    before = (lax.broadcasted_iota(I32, (TK, TK), 1) < lax.broadcasted_iota(I32, (TK, TK), 0))
    before = jnp.where(before, 1.0, 0.0).astype(BF16)

    def select_tile(kt, ties_so_far):
        k = keys_ref[tile(kt), :]
        tie = jnp.where(k == thr, 1.0, 0.0)
        rank = jnp.dot(before, tie.astype(BF16), preferred_element_type=F32) + ties_so_far
        sel = (k > thr) | ((k == thr) & (rank < need))
        sel_ref[tile(kt), :] = jnp.where(sel & (k != INT_MIN), 1.0, 0.0)
        return ties_so_far + jnp.sum(tie, axis=0, keepdims=True)

    lax.fori_loop(0, n_tiles, select_tile, jnp.zeros((1, TQ), F32))

    for h in range(N_HEADS):
        cols = slice(h * HEAD_DIM, (h + 1) * HEAD_DIM)
        q_h = qa_ref[:, cols]
        far_bias = relb_ref[FAR_BUCKET, h]

        def attend_tile(kt, carry, q_h=q_h, cols=cols, far_bias=far_bias, h=h):
            m, l, acc = carry
            logits = lax.dot_general(ka_ref[tile(kt), cols], q_h, NT_DIMS,
                                     preferred_element_type=F32) * HEAD_DIM ** -0.5
            dist = j - kt
            near = nb_ref[jnp.maximum(1 - dist, 0), h]
            logits = logits + jnp.where(dist >= 2, far_bias, near)
            sel = sel_ref[tile(kt), :]
            logits = jnp.where(sel > 0.0, logits, NEG_BIG)
            m_new = jnp.maximum(m, jnp.max(logits, axis=0, keepdims=True))
            alpha = jnp.exp(m - m_new)
            p = jnp.exp(logits - m_new) * sel
            l = l * alpha + jnp.sum(p, axis=0, keepdims=True)
            pv = lax.dot_general(va_ref[tile(kt), cols], p.astype(BF16), TN_DIMS,
                                 preferred_element_type=F32)
            return m_new, l, acc * alpha + pv

        init = (jnp.full((1, TQ), NEG_BIG, F32), jnp.zeros((1, TQ), F32),
                jnp.zeros((HEAD_DIM, TQ), F32))
        _, l, acc = lax.fori_loop(0, n_tiles, attend_tile, init)
        out_ref[0, cols, :] = (acc * (1.0 / l)).astype(BF16)


def _dsa(rel_bias, qkva, q_idx, k_idx, w_t, near_bias, batch, seq, n_sel):
    nq = seq // TQ
    q_row = lambda b, j: (b * nq + j, 0)
    return pl.pallas_call(
        functools.partial(_dsa_kernel, n_sel=n_sel),
        grid=(batch, nq),
        in_specs=[pl.BlockSpec(memory_space=pltpu.SMEM),
                  pl.BlockSpec((TQ, WIDTH), q_row),
                  pl.BlockSpec((seq, WIDTH), lambda b, j: (b, 1)),
                  pl.BlockSpec((seq, WIDTH), lambda b, j: (b, 2)),
                  pl.BlockSpec((TQ, WIDTH), q_row),
                  pl.BlockSpec((seq, IDX_DIM), lambda b, j: (b, 0)),
                  pl.BlockSpec((1, N_HEADS, TQ), lambda b, j: (b, 0, j)),
                  pl.BlockSpec(near_bias.shape, lambda b, j: (0, 0, 0, 0))],
        out_specs=pl.BlockSpec((1, WIDTH, TQ), lambda b, j: (b, 0, j)),
        out_shape=jax.ShapeDtypeStruct((batch, WIDTH, seq), BF16),
        scratch_shapes=[pltpu.VMEM((seq, TQ), I32), pltpu.VMEM((seq, TQ), F32)],
        compiler_params=_params(2),
        name="dsa",
    )(rel_bias, qkva, qkva, qkva, q_idx, k_idx, w_t, near_bias)


def _sb_kernel(q_ref, k_ref, v_ref, out_ref):
    j = pl.program_id(1)
    s_loc = lax.broadcasted_iota(I32, (TK, TQ), 0)
    t_loc = lax.broadcasted_iota(I32, (TK, TQ), 1)
    after = (lax.broadcasted_iota(I32, (TK, TK), 1) > lax.broadcasted_iota(I32, (TK, TK), 0))
    after = jnp.where(after, 1.0, 0.0).astype(BF16)

    for h in range(N_HEADS):
        cols = slice(h * HEAD_DIM, (h + 1) * HEAD_DIM)
        q_h = q_ref[:, cols]

        def live(carry):
            kt, kept, _ = carry
            return jnp.logical_and(kt >= 0, jnp.max(kept) > SB_UNDERFLOW)

        def tile_step(carry, q_h=q_h, cols=cols):
            kt, kept, acc = carry
            rows = pl.ds(pl.multiple_of(kt * TK, TK), TK)
            z = lax.dot_general(k_ref[rows, cols], q_h, NT_DIMS,
                                preferred_element_type=F32) * HEAD_DIM ** -0.5
            causal = (s_loc + kt * TK) < (t_loc + j * TQ)
            soft = jnp.log1p(jnp.exp(-jnp.abs(z)))
            log_beta = jnp.minimum(z, 0.0) - soft
            log_keep = jnp.where(causal, jnp.minimum(-z, 0.0) - soft, 0.0)
            hi = log_keep.astype(BF16)
            lo = (log_keep - hi.astype(F32)).astype(BF16)
            later = (jnp.dot(after, hi, preferred_element_type=F32)
                     + jnp.dot(after, lo, preferred_element_type=F32))
            a = jnp.where(causal, jnp.exp(log_beta + later + kept), 0.0)
            acc = acc + lax.dot_general(v_ref[rows, cols], a.astype(BF16), TN_DIMS,
                                        preferred_element_type=F32)
            return kt - 1, kept + jnp.sum(log_keep, axis=0, keepdims=True), acc

        init = (j, jnp.zeros((1, TQ), F32), jnp.zeros((HEAD_DIM, TQ), F32))
        _, _, acc = lax.while_loop(live, tile_step, init)
        out_ref[0, cols, :] = acc.astype(BF16)


def _sb(qkvs, batch, seq):
    nq = seq // TQ
    return pl.pallas_call(
        _sb_kernel,
        grid=(batch, nq),
        in_specs=[pl.BlockSpec((TQ, WIDTH), lambda b, j: (b * nq + j, 0)),
                  pl.BlockSpec((seq, WIDTH), lambda b, j: (b, 1)),
                  pl.BlockSpec((seq, WIDTH), lambda b, j: (b, 2))],
        out_specs=pl.BlockSpec((1, WIDTH, TQ), lambda b, j: (b, 0, j)),
        out_shape=jax.ShapeDtypeStruct((batch, WIDTH, seq), BF16),
        compiler_params=_params(2),
        name="sb",
    )(qkvs, qkvs, qkvs)


def _layer_norm(r, gain, bias):
    mu = jnp.mean(r, axis=-1, keepdims=True)
    c = r - mu
    var = jnp.mean(c * c, axis=-1, keepdims=True)
    return c * lax.rsqrt(var + LN_EPS) * gain + bias


def _merge_kernel(x_ref, ya_ref, ys_ref, wg_ref, bg_ref, wa_ref, wb_ref, wo_ref, g_ref, b_ref,
                  out_ref, *, alpha):
    x = x_ref[...]
    d = x.shape[1]
    gates = jnp.dot(x.astype(BF16), wg_ref[...], preferred_element_type=F32)
    gate_a = jax.nn.sigmoid(gates[:, :d] + bg_ref[0:1, :])
    gate_b = jax.nn.sigmoid(gates[:, d:] + bg_ref[1:2, :])
    branch_a = lax.dot_general(ya_ref[0], wa_ref[...], TN_DIMS, preferred_element_type=F32)
    branch_b = lax.dot_general(ys_ref[0], wb_ref[...], TN_DIMS, preferred_element_type=F32)
    merged = gate_a * branch_a + gate_b * branch_b
    mix = jnp.dot(merged.astype(BF16), wo_ref[...], preferred_element_type=F32)
    out_ref[...] = _layer_norm(alpha * x + mix, g_ref[...], b_ref[...])


def _merge(x2, ya_t, ys_t, wg, bg, wa, wb, wo, gain, bias, seq, tm, alpha):
    m, d = x2.shape
    per_seq = seq // tm
    const = lambda i: (0, 0)
    row = lambda i: (i, 0)
    y_blk = pl.BlockSpec((1, WIDTH, tm), lambda i: (i // per_seq, 0, i % per_seq))
    return pl.pallas_call(
        functools.partial(_merge_kernel, alpha=alpha),
        grid=(m // tm,),
        in_specs=[pl.BlockSpec((tm, d), row), y_blk, y_blk,
                  pl.BlockSpec(wg.shape, const), pl.BlockSpec(bg.shape, const),
                  pl.BlockSpec(wa.shape, const), pl.BlockSpec(wb.shape, const),
                  pl.BlockSpec(wo.shape, const), pl.BlockSpec(gain.shape, const),
                  pl.BlockSpec(bias.shape, const)],
        out_specs=pl.BlockSpec((tm, d), row),
        out_shape=jax.ShapeDtypeStruct((m, d), F32),
        compiler_params=_params(1),
        name="merge",
    )(x2, ya_t, ys_t, wg, bg, wa, wb, wo, gain, bias)


HALO = 8


def _gelu_tanh(x):
    return 0.5 * x * (1.0 + jnp.tanh(math.sqrt(2.0 / math.pi) * (x + 0.044715 * (x * x * x))))


def _ffn_kernel(x_ref, wu_ref, wv_ref, cw_ref, cb_ref, wo_ref, g_ref, b_ref, out_ref, u_ref,
                *, alpha, per_seq):
    tm = x_ref.shape[0]
    x = x_ref[...]
    xb = x.astype(BF16)

    @pl.when(pl.program_id(0) % per_seq == 0)
    def _():
        u_ref[0:HALO, :] = jnp.zeros((HALO, u_ref.shape[1]), F32)

    u_ref[HALO:HALO + tm, :] = jnp.dot(xb, wu_ref[...], preferred_element_type=F32)
    conv = (cw_ref[2:3, :] * u_ref[HALO:HALO + tm, :]
            + cw_ref[1:2, :] * u_ref[HALO - 1:HALO - 1 + tm, :]
            + cw_ref[0:1, :] * u_ref[HALO - 2:HALO - 2 + tm, :]
            + cb_ref[...])
    u_ref[0:HALO, :] = u_ref[tm:tm + HALO, :]
    gate = jnp.dot(xb, wv_ref[...], preferred_element_type=F32)
    hidden = (_gelu_tanh(conv) * gate).astype(BF16)
    y = jnp.dot(hidden, wo_ref[...], preferred_element_type=F32)
    out_ref[...] = _layer_norm(alpha * x + y, g_ref[...], b_ref[...])


def _ffn(x1, wu, wv, cw, cb, wo, gain, bias, seq, tm, alpha):
    m, d = x1.shape
    f = wu.shape[1]
    const = lambda i: (0, 0)
    row = lambda i: (i, 0)
    once = dict(pipeline_mode=pl.Buffered(1))
    return pl.pallas_call(
        functools.partial(_ffn_kernel, alpha=alpha, per_seq=seq // tm),
        grid=(m // tm,),
        in_specs=[pl.BlockSpec((tm, d), row),
                  pl.BlockSpec(wu.shape, const, **once), pl.BlockSpec(wv.shape, const, **once),
                  pl.BlockSpec(cw.shape, const), pl.BlockSpec(cb.shape, const),
                  pl.BlockSpec(wo.shape, const, **once),
                  pl.BlockSpec(gain.shape, const), pl.BlockSpec(bias.shape, const)],
        out_specs=pl.BlockSpec((tm, d), row),
        out_shape=jax.ShapeDtypeStruct((m, d), F32),
        scratch_shapes=[pltpu.VMEM((HALO + tm, f), F32)],
        compiler_params=_params(1),
        name="ffn",
    )(x1, wu, wv, cw, cb, wo, gain, bias)


def kernel(x, rel_bias, w_in, b_gates, w_branch_dsa, w_branch_sb, w_out, ln1_g, ln1_b,
           w_ffn_in, conv_w, conv_b, w_ffn_out, ln2_g, ln2_b):
    batch, seq, d = x.shape
    depth = w_in.shape[0]
    d_ff = conv_w.shape[-1]
    assert seq % TQ == 0 and TQ == TK and TQ % CHUNK == 0
    assert w_in.shape[-1] == 7 * WIDTH + IDX_DIM + N_HEADS + 2 * d
    alpha = (2.0 * depth) ** 0.25
    n_sel = min(TOPK_MAX, seq // 4)
    tm = min(512, seq)
    o_qi = 3 * WIDTH
    o_ki = o_qi + WIDTH
    o_wi = o_ki + IDX_DIM
    o_sb = o_wi + N_HEADS
    o_g = o_sb + 3 * WIDTH

    near_bias = _relbias_tiles(rel_bias)
    x2 = x.reshape(batch * seq, d)
    for layer in range(depth):
        w = w_in[layer].astype(BF16)
        qkva, q_idx, k_idx, w_t, qkvs = _proj(
            x2, w[:, :o_qi], w[:, o_qi:o_ki], w[:, o_ki:o_wi], w[:, o_wi:o_sb].T, w[:, o_sb:o_g],
            batch, seq, tm)
        ya_t = _dsa(rel_bias, qkva, q_idx, k_idx, w_t, near_bias, batch, seq, n_sel)
        ys_t = _sb(qkvs, batch, seq)
        x2 = _merge(x2, ya_t, ys_t, w[:, o_g:], b_gates[layer],
                    w_branch_dsa[layer].astype(BF16), w_branch_sb[layer].astype(BF16),
                    w_out[layer].astype(BF16), ln1_g[layer][None], ln1_b[layer][None],
                    seq, tm, alpha)
        w_ffn = w_ffn_in[layer].astype(BF16)
        x2 = _ffn(x2, w_ffn[:, :d_ff], w_ffn[:, d_ff:], conv_w[layer], conv_b[layer][None],
                  w_ffn_out[layer].astype(BF16), ln2_g[layer][None], ln2_b[layer][None],
                  seq, tm, alpha)
    return x2.reshape(batch, seq, d)
```

```python
import functools
import math

import jax
import jax.numpy as jnp
from jax import lax
from jax.experimental import pallas as pl
from jax.experimental.pallas import tpu as pltpu

F32 = jnp.float32
BF16 = jnp.bfloat16
I32 = jnp.int32

HEAD_DIM = 64
N_HEADS = 8
WIDTH = N_HEADS * HEAD_DIM
IDX_DIM = 64
CHUNK = 64
CHUNK_SHIFT = 6
TOPK_MAX = 256
N_REL_BUCKETS = 32
FAR_BUCKET = N_REL_BUCKETS // 2 - 1
CONV_WIDTH = 3
LN_EPS = 1e-5
TQ = 256
TK = 256
SUBLANES = 8
INT_MIN = -(2 ** 31)
NEG_BIG = -1e30
SB_UNDERFLOW = 104.0
VMEM_LIMIT = 56 * 1024 * 1024

NT_DIMS = (((1,), (1,)), ((), ()))
TN_DIMS = (((0,), (0,)), ((), ()))


def _params(n_axes, vmem=VMEM_LIMIT, **kw):
    return pltpu.CompilerParams(dimension_semantics=("arbitrary",) * n_axes,
                                vmem_limit_bytes=vmem, **kw)


def _key_tile(kt):
    return pl.ds(pl.multiple_of(kt * TK, TK), TK)


def _head_cols(h):
    return slice(h * HEAD_DIM, (h + 1) * HEAD_DIM)


def _proj_kernel(x_ref, wa_ref, wqi_ref, wki_ref, wwt_ref, ws_ref,
                 qkva_ref, qi_ref, ki_ref, wt_ref, qkvs_ref):
    xb = x_ref[...].astype(BF16)
    qkva_ref[...] = jnp.dot(xb, wa_ref[...], preferred_element_type=F32).astype(BF16)
    qi_ref[...] = jnp.dot(xb, wqi_ref[...], preferred_element_type=F32).astype(BF16)
    ki_ref[...] = jnp.dot(xb, wki_ref[...], preferred_element_type=F32).astype(BF16)
    wt_ref[0] = lax.dot_general(wwt_ref[...], xb, NT_DIMS, preferred_element_type=F32)
    qkvs_ref[...] = jnp.dot(xb, ws_ref[...], preferred_element_type=F32).astype(BF16)


def _proj(x2, wa, wqi, wki, wwt, ws, batch, seq, tm):
    m, d = x2.shape
    per_seq = seq // tm
    const = lambda i: (0, 0)
    row = lambda i: (i, 0)
    return pl.pallas_call(
        _proj_kernel,
        grid=(m // tm,),
        in_specs=[pl.BlockSpec((tm, d), row),
                  pl.BlockSpec(wa.shape, const), pl.BlockSpec(wqi.shape, const),
                  pl.BlockSpec(wki.shape, const), pl.BlockSpec(wwt.shape, const),
                  pl.BlockSpec(ws.shape, const)],
        out_specs=[pl.BlockSpec((tm, 3 * WIDTH), row), pl.BlockSpec((tm, WIDTH), row),
                   pl.BlockSpec((tm, IDX_DIM), row),
                   pl.BlockSpec((1, N_HEADS, tm), lambda i: (i // per_seq, 0, i % per_seq)),
                   pl.BlockSpec((tm, 3 * WIDTH), row)],
        out_shape=[jax.ShapeDtypeStruct((m, 3 * WIDTH), BF16),
                   jax.ShapeDtypeStruct((m, WIDTH), BF16),
                   jax.ShapeDtypeStruct((m, IDX_DIM), BF16),
                   jax.ShapeDtypeStruct((batch, N_HEADS, seq), F32),
                   jax.ShapeDtypeStruct((m, 3 * WIDTH), BF16)],
        compiler_params=_params(1),
        name="proj",
    )(x2, wa, wqi, wki, wwt, ws)


def _t5_bucket(rel):
    n = jnp.abs(rel)
    large = jnp.full(rel.shape, 8, I32)
    for edge in (12, 16, 23, 32, 46, 64, 91):
        large = large + jnp.where(n >= edge, 1, 0)
    return jnp.where(rel > 0, N_REL_BUCKETS // 2, 0) + jnp.where(n < 8, n, large)


def _relbias_kernel(relb_ref, out_ref):
    o = pl.program_id(0)
    h = pl.program_id(1)
    s = lax.broadcasted_iota(I32, (TK, TQ), 0)
    t = lax.broadcasted_iota(I32, (TK, TQ), 1)
    bucket = _t5_bucket((o - 1) * TK + s - t)
    acc = jnp.zeros((TK, TQ), F32)
    for b in range(N_REL_BUCKETS):
        acc = jnp.where(bucket == b, relb_ref[b, h], acc)
    out_ref[0, 0] = acc


def _relbias_tiles(rel_bias):
    return pl.pallas_call(
        _relbias_kernel,
        grid=(2, N_HEADS),
        in_specs=[pl.BlockSpec(memory_space=pltpu.SMEM)],
        out_specs=pl.BlockSpec((1, 1, TK, TQ), lambda o, h: (o, h, 0, 0)),
        out_shape=jax.ShapeDtypeStruct((2, N_HEADS, TK, TQ), F32),
        compiler_params=_params(2),
        name="relbias",
    )(rel_bias)


def _dsa_kernel(relb_ref, qa_ref, ka_ref, va_ref, qi_ref, ki_ref, wt_ref, nb_ref,
                out_ref, keys_ref, sel_ref, ml_ref, acc_ref, *, n_sel):
    j = pl.program_id(1)
    n_tiles = j + 1
    s_loc = lax.broadcasted_iota(I32, (TK, TQ), 0)
    t_loc = lax.broadcasted_iota(I32, (TK, TQ), 1)
    q_chunk = (t_loc + j * TQ) >> CHUNK_SHIFT

    w_scaled = wt_ref[0] * N_HEADS ** -0.5

    def score_tile(kt, carry):
        k_idx = ki_ref[_key_tile(kt), :]
        acc = jnp.zeros((TK, TQ), F32)
        for h in range(N_HEADS):
            d = lax.dot_general(k_idx, qi_ref[:, h * IDX_DIM:(h + 1) * IDX_DIM], NT_DIMS,
                                preferred_element_type=F32)
            acc = acc + w_scaled[h:h + 1, :] * jnp.maximum(d, 0.0)
        bits = pltpu.bitcast(acc, I32)
        key = jnp.where(bits < 0, bits ^ 0x7FFFFFFF, bits)
        valid = ((s_loc + kt * TK) >> CHUNK_SHIFT) <= q_chunk
        keys_ref[_key_tile(kt), :] = jnp.where(valid, key, INT_MIN)
        return carry

    lax.fori_loop(0, n_tiles, score_tile, 0)

    def count(pred_of_keys):
        def body(kt, partial):
            hit = jnp.where(pred_of_keys(keys_ref[_key_tile(kt), :]), 1.0, 0.0)
            return partial + jnp.sum(hit.reshape(TK // SUBLANES, SUBLANES, TQ), axis=0)
        partial = lax.fori_loop(0, n_tiles, body, jnp.zeros((SUBLANES, TQ), F32))
        return jnp.sum(partial, axis=0, keepdims=True)

    def bit_step(i, thr):
        cand = thr ^ lax.shift_left(jnp.int32(1), 31 - i)
        return jnp.where(count(lambda k: k >= cand) >= n_sel, cand, thr)

    thr = lax.fori_loop(0, 32, bit_step, jnp.full((1, TQ), INT_MIN, I32))

    need = n_sel - count(lambda k: k > thr)
    before = (lax.broadcasted_iota(I32, (TK, TK), 1) < lax.broadcasted_iota(I32, (TK, TK), 0))
    before = jnp.where(before, 1.0, 0.0).astype(BF16)

    def select_tile(kt, ties_so_far):
        k = keys_ref[_key_tile(kt), :]
        tie = jnp.where(k == thr, 1.0, 0.0)
        rank = jnp.dot(before, tie.astype(BF16), preferred_element_type=F32) + ties_so_far
        sel = (k > thr) | ((k == thr) & (rank < need))
        sel_ref[_key_tile(kt), :] = jnp.where(sel & (k != INT_MIN), 0.0, -jnp.inf)
        return ties_so_far + jnp.sum(tie, axis=0, keepdims=True)

    lax.fori_loop(0, n_tiles, select_tile, jnp.zeros((1, TQ), F32))

    ml_ref[0:N_HEADS, :] = jnp.full((N_HEADS, TQ), NEG_BIG, F32)
    ml_ref[N_HEADS:2 * N_HEADS, :] = jnp.zeros((N_HEADS, TQ), F32)
    acc_ref[...] = jnp.zeros(acc_ref.shape, F32)

    def attend(kt, near):
        rows = _key_tile(kt)
        masked = sel_ref[rows, :]
        for h in range(N_HEADS):
            cols = _head_cols(h)
            logits = lax.dot_general(ka_ref[rows, cols], qa_ref[:, cols], NT_DIMS,
                                     preferred_element_type=F32) + masked
            if near is None:
                shift = relb_ref[FAR_BUCKET, h]
            else:
                logits = logits + nb_ref[near, h]
                shift = 0.0
            m_old = ml_ref[h:h + 1, :]
            m_new = jnp.maximum(m_old, jnp.max(logits, axis=0, keepdims=True) + shift)
            alpha = jnp.exp(m_old - m_new)
            p = jnp.exp(logits - (m_new - shift))
            ml_ref[h:h + 1, :] = m_new
            ml_ref[N_HEADS + h:N_HEADS + h + 1, :] = (
                ml_ref[N_HEADS + h:N_HEADS + h + 1, :] * alpha + jnp.sum(p, axis=0, keepdims=True))
            pv = lax.dot_general(va_ref[rows, cols], p.astype(BF16), TN_DIMS,
                                 preferred_element_type=F32)
            acc_ref[cols, :] = acc_ref[cols, :] * alpha + pv

    def far_tile(kt, carry):
        attend(kt, None)
        return carry

    lax.fori_loop(0, jnp.maximum(j - 1, 0), far_tile, 0)

    @pl.when(j >= 1)
    def _():
        attend(j - 1, 0)

    attend(j, 1)

    for h in range(N_HEADS):
        cols = _head_cols(h)
        inv = 1.0 / ml_ref[N_HEADS + h:N_HEADS + h + 1, :]
        out_ref[0, cols, :] = (acc_ref[cols, :] * inv).astype(BF16)


def _dsa(rel_bias, qkva, q_idx, k_idx, w_t, near_bias, batch, seq, n_sel):
    nq = seq // TQ
    q_row = lambda b, j: (b * nq + j, 0)
    return pl.pallas_call(
        functools.partial(_dsa_kernel, n_sel=n_sel),
        grid=(batch, nq),
        in_specs=[pl.BlockSpec(memory_space=pltpu.SMEM),
                  pl.BlockSpec((TQ, WIDTH), q_row),
                  pl.BlockSpec((seq, WIDTH), lambda b, j: (b, 1)),
                  pl.BlockSpec((seq, WIDTH), lambda b, j: (b, 2)),
                  pl.BlockSpec((TQ, WIDTH), q_row),
                  pl.BlockSpec((seq, IDX_DIM), lambda b, j: (b, 0)),
                  pl.BlockSpec((1, N_HEADS, TQ), lambda b, j: (b, 0, j)),
                  pl.BlockSpec(near_bias.shape, lambda b, j: (0, 0, 0, 0))],
        out_specs=pl.BlockSpec((1, WIDTH, TQ), lambda b, j: (b, 0, j)),
        out_shape=jax.ShapeDtypeStruct((batch, WIDTH, seq), BF16),
        scratch_shapes=[pltpu.VMEM((seq, TQ), I32), pltpu.VMEM((seq, TQ), F32),
                        pltpu.VMEM((2 * N_HEADS, TQ), F32), pltpu.VMEM((WIDTH, TQ), F32)],
        compiler_params=_params(2),
        name="dsa",
    )(rel_bias, qkva, qkva, qkva, q_idx, k_idx, w_t, near_bias)


def _sb_kernel(q_ref, k_ref, v_ref, out_ref, gone_ref, acc_ref):
    j = pl.program_id(1)
    s_loc = lax.broadcasted_iota(I32, (TK, TQ), 0)
    t_loc = lax.broadcasted_iota(I32, (TK, TQ), 1)
    causal = s_loc < t_loc
    row = lax.broadcasted_iota(I32, (TK, 2 * TK), 0)
    col = lax.broadcasted_iota(I32, (TK, 2 * TK), 1)
    after2 = jnp.where(jnp.where(col >= TK, col - TK, col) > row, 1.0, 0.0).astype(BF16)

    gone_ref[...] = jnp.zeros(gone_ref.shape, F32)
    acc_ref[...] = jnp.zeros(acc_ref.shape, F32)

    def step(kt, diagonal):
        rows = _key_tile(kt)
        for h in range(N_HEADS):
            cols = _head_cols(h)
            z = lax.dot_general(k_ref[rows, cols], q_ref[:, cols], NT_DIMS,
                                preferred_element_type=F32)
            soft = jnp.log(1.0 + jnp.exp(-jnp.abs(z)))
            log_beta = jnp.minimum(z, 0.0) - soft
            drop = jnp.maximum(z, 0.0) + soft
            if diagonal:
                drop = jnp.where(causal, drop, 0.0)
            hi = drop.astype(BF16)
            lo = (drop - hi.astype(F32)).astype(BF16)
            below = jnp.dot(after2, jnp.concatenate([hi, lo], axis=0),
                            preferred_element_type=F32)
            gone = gone_ref[h:h + 1, :]
            a = jnp.exp(log_beta - below - gone)
            if diagonal:
                a = jnp.where(causal, a, 0.0)
            acc_ref[cols, :] += lax.dot_general(v_ref[rows, cols], a.astype(BF16), TN_DIMS,
                                                preferred_element_type=F32)
            gone_ref[h:h + 1, :] = gone + below[0:1, :] + drop[0:1, :]

    def still_live():
        return (jnp.min(gone_ref[...]) <= SB_UNDERFLOW).astype(I32)

    step(j, True)

    def walk(carry):
        kt, _ = carry
        step(kt, False)
        return kt - 1, still_live()

    lax.while_loop(lambda c: jnp.logical_and(c[0] >= 0, c[1] > 0), walk, (j - 1, still_live()))
    out_ref[0] = acc_ref[...].astype(BF16)


def _sb(qkvs, batch, seq):
    nq = seq // TQ
    return pl.pallas_call(
        _sb_kernel,
        grid=(batch, nq),
        in_specs=[pl.BlockSpec((TQ, WIDTH), lambda b, j: (b * nq + j, 0)),
                  pl.BlockSpec((seq, WIDTH), lambda b, j: (b, 1)),
                  pl.BlockSpec((seq, WIDTH), lambda b, j: (b, 2))],
        out_specs=pl.BlockSpec((1, WIDTH, TQ), lambda b, j: (b, 0, j)),
        out_shape=jax.ShapeDtypeStruct((batch, WIDTH, seq), BF16),
        scratch_shapes=[pltpu.VMEM((N_HEADS, TQ), F32), pltpu.VMEM((WIDTH, TQ), F32)],
        compiler_params=_params(2),
        name="sb",
    )(qkvs, qkvs, qkvs)


def _layer_norm(r, gain, bias):
    mu = jnp.mean(r, axis=-1, keepdims=True)
    c = r - mu
    var = jnp.mean(c * c, axis=-1, keepdims=True)
    return c * lax.rsqrt(var + LN_EPS) * gain + bias


def _merge_kernel(x_ref, ya_ref, ys_ref, wg_ref, bg_ref, wa_ref, wb_ref, wo_ref, g_ref, b_ref,
                  out_ref, *, alpha):
    x = x_ref[...]
    d = x.shape[1]
    gates = jnp.dot(x.astype(BF16), wg_ref[...], preferred_element_type=F32)
    gate_a = jax.nn.sigmoid(gates[:, :d] + bg_ref[0:1, :])
    gate_b = jax.nn.sigmoid(gates[:, d:] + bg_ref[1:2, :])
    branch_a = lax.dot_general(ya_ref[0], wa_ref[...], TN_DIMS, preferred_element_type=F32)
    branch_b = lax.dot_general(ys_ref[0], wb_ref[...], TN_DIMS, preferred_element_type=F32)
    merged = gate_a * branch_a + gate_b * branch_b
    mix = jnp.dot(merged.astype(BF16), wo_ref[...], preferred_element_type=F32)
    out_ref[...] = _layer_norm(alpha * x + mix, g_ref[...], b_ref[...])


def _merge(x2, ya_t, ys_t, wg, bg, wa, wb, wo, gain, bias, seq, tm, alpha):
    m, d = x2.shape
    per_seq = seq // tm
    const = lambda i: (0, 0)
    row = lambda i: (i, 0)
    y_blk = pl.BlockSpec((1, WIDTH, tm), lambda i: (i // per_seq, 0, i % per_seq))
    return pl.pallas_call(
        functools.partial(_merge_kernel, alpha=alpha),
        grid=(m // tm,),
        in_specs=[pl.BlockSpec((tm, d), row), y_blk, y_blk,
                  pl.BlockSpec(wg.shape, const), pl.BlockSpec(bg.shape, const),
                  pl.BlockSpec(wa.shape, const), pl.BlockSpec(wb.shape, const),
                  pl.BlockSpec(wo.shape, const), pl.BlockSpec(gain.shape, const),
                  pl.BlockSpec(bias.shape, const)],
        out_specs=pl.BlockSpec((tm, d), row),
        out_shape=jax.ShapeDtypeStruct((m, d), F32),
        compiler_params=_params(1),
        name="merge",
    )(x2, ya_t, ys_t, wg, bg, wa, wb, wo, gain, bias)


HALO = 8


def _gelu_tanh(x):
    return 0.5 * x * (1.0 + jnp.tanh(math.sqrt(2.0 / math.pi) * (x + 0.044715 * (x * x * x))))


def _ffn_kernel(x_ref, wu_ref, wv_ref, cw_ref, cb_ref, wo_ref, g_ref, b_ref, out_ref, u_ref,
                *, alpha, per_seq):
    tm = x_ref.shape[0]
    x = x_ref[...]
    xb = x.astype(BF16)

    @pl.when(pl.program_id(0) % per_seq == 0)
    def _():
        u_ref[0:HALO, :] = jnp.zeros((HALO, u_ref.shape[1]), F32)

    u_ref[HALO:HALO + tm, :] = jnp.dot(xb, wu_ref[...], preferred_element_type=F32)
    conv = (cw_ref[2:3, :] * u_ref[HALO:HALO + tm, :]
            + cw_ref[1:2, :] * u_ref[HALO - 1:HALO - 1 + tm, :]
            + cw_ref[0:1, :] * u_ref[HALO - 2:HALO - 2 + tm, :]
            + cb_ref[...])
    u_ref[0:HALO, :] = u_ref[tm:tm + HALO, :]
    gate = jnp.dot(xb, wv_ref[...], preferred_element_type=F32)
    hidden = (_gelu_tanh(conv) * gate).astype(BF16)
    y = jnp.dot(hidden, wo_ref[...], preferred_element_type=F32)
    out_ref[...] = _layer_norm(alpha * x + y, g_ref[...], b_ref[...])


def _ffn(x1, wu, wv, cw, cb, wo, gain, bias, seq, tm, alpha):
    m, d = x1.shape
    f = wu.shape[1]
    const = lambda i: (0, 0)
    row = lambda i: (i, 0)
    once = dict(pipeline_mode=pl.Buffered(1))
    return pl.pallas_call(
        functools.partial(_ffn_kernel, alpha=alpha, per_seq=seq // tm),
        grid=(m // tm,),
        in_specs=[pl.BlockSpec((tm, d), row),
                  pl.BlockSpec(wu.shape, const, **once), pl.BlockSpec(wv.shape, const, **once),
                  pl.BlockSpec(cw.shape, const), pl.BlockSpec(cb.shape, const),
                  pl.BlockSpec(wo.shape, const, **once),
                  pl.BlockSpec(gain.shape, const), pl.BlockSpec(bias.shape, const)],
        out_specs=pl.BlockSpec((tm, d), row),
        out_shape=jax.ShapeDtypeStruct((m, d), F32),
        scratch_shapes=[pltpu.VMEM((HALO + tm, f), F32)],
        compiler_params=_params(1),
        name="ffn",
    )(x1, wu, wv, cw, cb, wo, gain, bias)


def kernel(x, rel_bias, w_in, b_gates, w_branch_dsa, w_branch_sb, w_out, ln1_g, ln1_b,
           w_ffn_in, conv_w, conv_b, w_ffn_out, ln2_g, ln2_b):
    batch, seq, d = x.shape
    depth = w_in.shape[0]
    d_ff = conv_w.shape[-1]
    assert seq % TQ == 0 and TQ == TK and TQ % CHUNK == 0
    assert w_in.shape[-1] == 7 * WIDTH + IDX_DIM + N_HEADS + 2 * d
    alpha = (2.0 * depth) ** 0.25
    n_sel = min(TOPK_MAX, seq // 4)
    tm = min(512, seq)
    o_qi = 3 * WIDTH
    o_ki = o_qi + WIDTH
    o_wi = o_ki + IDX_DIM
    o_sb = o_wi + N_HEADS
    o_g = o_sb + 3 * WIDTH
    q_scale = jnp.ones((w_in.shape[-1],), F32)
    q_scale = q_scale.at[0:WIDTH].set(HEAD_DIM ** -0.5)
    q_scale = q_scale.at[o_qi:o_ki].set(IDX_DIM ** -0.5)
    q_scale = q_scale.at[o_sb:o_sb + WIDTH].set(HEAD_DIM ** -0.5)

    near_bias = _relbias_tiles(rel_bias)
    x2 = x.reshape(batch * seq, d)
    for layer in range(depth):
        w = (w_in[layer] * q_scale).astype(BF16)
        qkva, q_idx, k_idx, w_t, qkvs = _proj(
            x2, w[:, :o_qi], w[:, o_qi:o_ki], w[:, o_ki:o_wi], w[:, o_wi:o_sb].T, w[:, o_sb:o_g],
            batch, seq, tm)
        ya_t = _dsa(rel_bias, qkva, q_idx, k_idx, w_t, near_bias, batch, seq, n_sel)
        ys_t = _sb(qkvs, batch, seq)
        x2 = _merge(x2, ya_t, ys_t, w[:, o_g:], b_gates[layer],
                    w_branch_dsa[layer].astype(BF16), w_branch_sb[layer].astype(BF16),
                    w_out[layer].astype(BF16), ln1_g[layer][None], ln1_b[layer][None],
                    seq, tm, alpha)
        w_ffn = w_ffn_in[layer].astype(BF16)
        x2 = _ffn(x2, w_ffn[:, :d_ff], w_ffn[:, d_ff:], conv_w[layer], conv_b[layer][None],
                  w_ffn_out[layer].astype(BF16), ln2_g[layer][None], ln2_b[layer][None],
                  seq, tm, alpha)
    return x2.reshape(batch, seq, d)
```

```python
import functools
import math

import jax
import jax.numpy as jnp
from jax import lax
from jax.experimental import pallas as pl
from jax.experimental.pallas import tpu as pltpu

F32 = jnp.float32
BF16 = jnp.bfloat16
I32 = jnp.int32
I16 = jnp.int16

HEAD_DIM = 64
N_HEADS = 8
WIDTH = N_HEADS * HEAD_DIM
IDX_DIM = 64
CHUNK = 64
CHUNK_SHIFT = 6
TOPK_MAX = 256
N_REL_BUCKETS = 32
FAR_BUCKET = N_REL_BUCKETS // 2 - 1
CONV_WIDTH = 3
LN_EPS = 1e-5
TQ = 256
TK = 256
TKS = 128
SUBLANES = 8
PACKED_ROWS = 16
HALF_RANGE = 2 ** 15
INT_MIN = -(2 ** 31)
NEG_BIG = -1e30
LOG2_E = math.log2(math.e)
SB_UNDERFLOW = 151.0
VMEM_LIMIT = 56 * 1024 * 1024

NT_DIMS = (((1,), (1,)), ((), ()))
TN_DIMS = (((0,), (0,)), ((), ()))


def _params(n_axes, vmem=VMEM_LIMIT, **kw):
    return pltpu.CompilerParams(dimension_semantics=("arbitrary",) * n_axes,
                                vmem_limit_bytes=vmem, **kw)


def _key_tile(kt):
    return pl.ds(pl.multiple_of(kt * TK, TK), TK)


def _head_cols(h):
    return slice(h * HEAD_DIM, (h + 1) * HEAD_DIM)


def _proj_kernel(x_ref, wa_ref, wqi_ref, wki_ref, wwt_ref, ws_ref,
                 qkva_ref, qi_ref, ki_ref, wt_ref, qkvs_ref):
    xb = x_ref[...].astype(BF16)
    qkva_ref[...] = jnp.dot(xb, wa_ref[...], preferred_element_type=F32).astype(BF16)
    qi_ref[...] = jnp.dot(xb, wqi_ref[...], preferred_element_type=F32).astype(BF16)
    ki_ref[...] = jnp.dot(xb, wki_ref[...], preferred_element_type=F32).astype(BF16)
    wt_ref[0] = lax.dot_general(wwt_ref[...], xb, NT_DIMS, preferred_element_type=F32)
    qkvs_ref[...] = jnp.dot(xb, ws_ref[...], preferred_element_type=F32).astype(BF16)


def _proj(x2, wa, wqi, wki, wwt, ws, batch, seq, tm):
    m, d = x2.shape
    per_seq = seq // tm
    const = lambda i: (0, 0)
    row = lambda i: (i, 0)
    return pl.pallas_call(
        _proj_kernel,
        grid=(m // tm,),
        in_specs=[pl.BlockSpec((tm, d), row),
                  pl.BlockSpec(wa.shape, const), pl.BlockSpec(wqi.shape, const),
                  pl.BlockSpec(wki.shape, const), pl.BlockSpec(wwt.shape, const),
                  pl.BlockSpec(ws.shape, const)],
        out_specs=[pl.BlockSpec((tm, 3 * WIDTH), row), pl.BlockSpec((tm, WIDTH), row),
                   pl.BlockSpec((tm, IDX_DIM), row),
                   pl.BlockSpec((1, N_HEADS, tm), lambda i: (i // per_seq, 0, i % per_seq)),
                   pl.BlockSpec((tm, 3 * WIDTH), row)],
        out_shape=[jax.ShapeDtypeStruct((m, 3 * WIDTH), BF16),
                   jax.ShapeDtypeStruct((m, WIDTH), BF16),
                   jax.ShapeDtypeStruct((m, IDX_DIM), BF16),
                   jax.ShapeDtypeStruct((batch, N_HEADS, seq), F32),
                   jax.ShapeDtypeStruct((m, 3 * WIDTH), BF16)],
        compiler_params=_params(1),
        name="proj",
    )(x2, wa, wqi, wki, wwt, ws)


def _t5_bucket(rel):
    n = jnp.abs(rel)
    large = jnp.full(rel.shape, 8, I32)
    for edge in (12, 16, 23, 32, 46, 64, 91):
        large = large + jnp.where(n >= edge, 1, 0)
    return jnp.where(rel > 0, N_REL_BUCKETS // 2, 0) + jnp.where(n < 8, n, large)


def _relbias_kernel(relb_ref, out_ref):
    o = pl.program_id(0)
    h = pl.program_id(1)
    s = lax.broadcasted_iota(I32, (TK, TQ), 0)
    t = lax.broadcasted_iota(I32, (TK, TQ), 1)
    bucket = _t5_bucket((o - 1) * TK + s - t)
    acc = jnp.zeros((TK, TQ), F32)
    for b in range(N_REL_BUCKETS):
        acc = jnp.where(bucket == b, relb_ref[b, h], acc)
    out_ref[0, 0] = acc


def _relbias_tiles(rel_bias):
    return pl.pallas_call(
        _relbias_kernel,
        grid=(2, N_HEADS),
        in_specs=[pl.BlockSpec(memory_space=pltpu.SMEM)],
        out_specs=pl.BlockSpec((1, 1, TK, TQ), lambda o, h: (o, h, 0, 0)),
        out_shape=jax.ShapeDtypeStruct((2, N_HEADS, TK, TQ), F32),
        compiler_params=_params(2),
        name="relbias",
    )(rel_bias)


def _dsa_kernel(relb_ref, qa_ref, ka_ref, va_ref, qi_ref, ki_ref, wt_ref, nb_ref,
                out_ref, keys_ref, hi_ref, lo_ref, sel_ref, mx_ref, sum_ref, acc_ref, p_ref,
                lg_ref, *, n_sel):
    j = pl.program_id(1)
    n_tiles = j + 1
    s_loc = lax.broadcasted_iota(I32, (TK, TQ), 0)
    t_loc = lax.broadcasted_iota(I32, (TK, TQ), 1)
    q_chunk = (t_loc + j * TQ) >> CHUNK_SHIFT

    w_scaled = wt_ref[0] * N_HEADS ** -0.5

    def score_tile(kt, carry):
        k_idx = ki_ref[_key_tile(kt), :]
        acc = jnp.zeros((TK, TQ), F32)
        for h in range(N_HEADS):
            d = lax.dot_general(k_idx, qi_ref[:, h * IDX_DIM:(h + 1) * IDX_DIM], NT_DIMS,
                                preferred_element_type=F32)
            acc = acc + w_scaled[h:h + 1, :] * jnp.maximum(d, 0.0)
        bits = pltpu.bitcast(acc, I32)
        key = jnp.where(bits < 0, bits ^ 0x7FFFFFFF, bits)
        valid = ((s_loc + kt * TK) >> CHUNK_SHIFT) <= q_chunk
        key = jnp.where(valid, key, INT_MIN)
        keys_ref[_key_tile(kt), :] = key
        hi_ref[_key_tile(kt), :] = (key >> 16).astype(I16)
        lo_ref[_key_tile(kt), :] = ((key & 0xFFFF) - HALF_RANGE).astype(I16)
        return carry

    lax.fori_loop(0, n_tiles, score_tile, 0)

    def count16(ref, level, pred):
        level16 = jnp.broadcast_to(level, (PACKED_ROWS, TQ)).astype(I16)

        def body(kt, partial):
            tile = ref[_key_tile(kt), :].reshape(TK // PACKED_ROWS, PACKED_ROWS, TQ)
            hit = jnp.where(pred(tile, level16[None]), jnp.int16(1), jnp.int16(0))
            for r in range(TK // PACKED_ROWS):
                partial = partial + hit[r]
            return partial

        partial = lax.fori_loop(0, n_tiles, body, jnp.zeros((PACKED_ROWS, TQ), I16))
        return jnp.sum(partial.astype(I32), axis=0, keepdims=True)

    def search16(ref, want):
        def bit_step(i, level):
            cand = level ^ lax.shift_left(jnp.int32(1), 15 - i)
            cand = jnp.where(i == 0, level + HALF_RANGE, cand)
            return jnp.where(count16(ref, cand, lambda a, b: a >= b) >= want, cand, level)
        return lax.fori_loop(0, 16, bit_step, jnp.full((1, TQ), -HALF_RANGE, I32))

    thr_hi = search16(hi_ref, n_sel)
    want_lo = n_sel - count16(hi_ref, thr_hi, lambda a, b: a > b)
    thr_hi16 = jnp.broadcast_to(thr_hi, (PACKED_ROWS, TQ)).astype(I16)

    def bucket_tile(kt, carry):
        hi = hi_ref[_key_tile(kt), :].reshape(TK // PACKED_ROWS, PACKED_ROWS, TQ)
        lo = lo_ref[_key_tile(kt), :].reshape(TK // PACKED_ROWS, PACKED_ROWS, TQ)
        lo = jnp.where(hi == thr_hi16[None], lo, jnp.int16(-HALF_RANGE))
        lo_ref[_key_tile(kt), :] = lo.reshape(TK, TQ)
        return carry

    lax.fori_loop(0, n_tiles, bucket_tile, 0)
    thr_lo = search16(lo_ref, want_lo)
    thr = thr_hi * (2 * HALF_RANGE) + (thr_lo + HALF_RANGE)

    need = (want_lo - count16(lo_ref, thr_lo, lambda a, b: a > b)).astype(F32)
    before = (lax.broadcasted_iota(I32, (TK, TK), 1) < lax.broadcasted_iota(I32, (TK, TK), 0))
    before = jnp.where(before, 1.0, 0.0).astype(BF16)

    def select_tile(kt, ties_so_far):
        k = keys_ref[_key_tile(kt), :]
        tie = jnp.where(k == thr, 1.0, 0.0)
        rank = jnp.dot(before, tie.astype(BF16), preferred_element_type=F32) + ties_so_far
        sel = (k > thr) | ((k == thr) & (rank < need))
        sel_ref[_key_tile(kt), :] = jnp.where(sel & (k != INT_MIN), 0.0, -jnp.inf)
        return ties_so_far + jnp.sum(tie, axis=0, keepdims=True)

    lax.fori_loop(0, n_tiles, select_tile, jnp.zeros((1, TQ), F32))

    def logits_of(kt, h, near):
        cols = _head_cols(h)
        x = lax.dot_general(ka_ref[_key_tile(kt), cols], qa_ref[:, cols], NT_DIMS,
                            preferred_element_type=F32) + sel_ref[_key_tile(kt), :]
        return x if near is None else x + nb_ref[near, h]

    def fold(x, op):
        return op(x.reshape(TK // SUBLANES, SUBLANES, TQ), axis=0)

    mx_ref[...] = jnp.full(mx_ref.shape, -jnp.inf, F32)

    def max_pass(kt, near):
        for h in range(N_HEADS):
            part = 0 if near is None else 1
            x = logits_of(kt, h, near)
            lg_ref[kt, h] = x
            mx_ref[part, h] = jnp.maximum(mx_ref[part, h], fold(x, jnp.max))

    def far_max(kt, carry):
        max_pass(kt, None)
        return carry

    n_far = jnp.maximum(j - 1, 0)
    lax.fori_loop(0, n_far, far_max, 0)

    @pl.when(j >= 1)
    def _():
        max_pass(j - 1, 0)

    max_pass(j, 1)

    m_all = []
    for h in range(N_HEADS):
        far = jnp.max(mx_ref[0, h], axis=0, keepdims=True) + relb_ref[FAR_BUCKET, h]
        near = jnp.max(mx_ref[1, h], axis=0, keepdims=True)
        m_all.append(jnp.maximum(jnp.maximum(far, near), NEG_BIG))
    sum_ref[...] = jnp.zeros(sum_ref.shape, F32)
    acc_ref[...] = jnp.zeros(acc_ref.shape, F32)

    def exp_pass(kt, near):
        for h in range(N_HEADS):
            shift = m_all[h] - (relb_ref[FAR_BUCKET, h] if near is None else 0.0)
            p = jnp.exp(lg_ref[kt, h] - shift)
            sum_ref[h] = sum_ref[h] + fold(p, jnp.sum)
            p_ref[h] = p.astype(BF16)
        for h in range(N_HEADS):
            cols = _head_cols(h)
            acc_ref[cols, :] += lax.dot_general(va_ref[_key_tile(kt), cols], p_ref[h], TN_DIMS,
                                                preferred_element_type=F32)

    def far_exp(kt, carry):
        exp_pass(kt, None)
        return carry

    lax.fori_loop(0, n_far, far_exp, 0)

    @pl.when(j >= 1)
    def _():
        exp_pass(j - 1, 0)

    exp_pass(j, 1)

    for h in range(N_HEADS):
        cols = _head_cols(h)
        inv = 1.0 / jnp.sum(sum_ref[h], axis=0, keepdims=True)
        out_ref[0, cols, :] = (acc_ref[cols, :] * inv).astype(BF16)


def _dsa(rel_bias, qkva, q_idx, k_idx, w_t, near_bias, batch, seq, n_sel):
    nq = seq // TQ
    q_row = lambda b, j: (b * nq + j, 0)
    return pl.pallas_call(
        functools.partial(_dsa_kernel, n_sel=n_sel),
        grid=(batch, nq),
        in_specs=[pl.BlockSpec(memory_space=pltpu.SMEM),
                  pl.BlockSpec((TQ, WIDTH), q_row),
                  pl.BlockSpec((seq, WIDTH), lambda b, j: (b, 1)),
                  pl.BlockSpec((seq, WIDTH), lambda b, j: (b, 2)),
                  pl.BlockSpec((TQ, WIDTH), q_row),
                  pl.BlockSpec((seq, IDX_DIM), lambda b, j: (b, 0)),
                  pl.BlockSpec((1, N_HEADS, TQ), lambda b, j: (b, 0, j)),
                  pl.BlockSpec(near_bias.shape, lambda b, j: (0, 0, 0, 0),
                               pipeline_mode=pl.Buffered(1))],
        out_specs=pl.BlockSpec((1, WIDTH, TQ), lambda b, j: (b, 0, j)),
        out_shape=jax.ShapeDtypeStruct((batch, WIDTH, seq), BF16),
        scratch_shapes=[pltpu.VMEM((seq, TQ), I32), pltpu.VMEM((seq, TQ), I16),
                        pltpu.VMEM((seq, TQ), I16), pltpu.VMEM((seq, TQ), F32),
                        pltpu.VMEM((2, N_HEADS, SUBLANES, TQ), F32),
                        pltpu.VMEM((N_HEADS, SUBLANES, TQ), F32),
                        pltpu.VMEM((WIDTH, TQ), F32), pltpu.VMEM((N_HEADS, TK, TQ), BF16),
                        pltpu.VMEM((seq // TK, N_HEADS, TK, TQ), F32)],
        compiler_params=_params(2),
        name="dsa",
    )(rel_bias, qkva, qkva, qkva, q_idx, k_idx, w_t, near_bias)


def _sb_kernel(q_ref, k_ref, v_ref, out_ref, gone_ref, acc_ref, lb_ref, hl_ref, a_ref):
    j = pl.program_id(1)
    s_loc = lax.broadcasted_iota(I32, (TKS, TQ), 0)
    t_loc = lax.broadcasted_iota(I32, (TKS, TQ), 1)
    row = lax.broadcasted_iota(I32, (TKS, 2 * TKS), 0)
    col = lax.broadcasted_iota(I32, (TKS, 2 * TKS), 1)
    after2 = jnp.where(jnp.where(col >= TKS, col - TKS, col) > row, 1.0, 0.0).astype(BF16)

    gone_ref[...] = jnp.zeros(gone_ref.shape, F32)
    acc_ref[...] = jnp.zeros(acc_ref.shape, F32)

    def step(ks, diagonal):
        rows = pl.ds(pl.multiple_of(ks * TKS, TKS), TKS)
        if diagonal:
            causal = (s_loc + ks * TKS) < (t_loc + j * TQ)
        for h in range(N_HEADS):
            cols = _head_cols(h)
            z2 = lax.dot_general(k_ref[rows, cols], q_ref[:, cols], NT_DIMS,
                                 preferred_element_type=F32) * LOG2_E
            neg_abs = pltpu.bitcast(pltpu.bitcast(z2, I32) | INT_MIN, F32)
            soft = jnp.log2(1.0 + jnp.exp2(neg_abs))
            lb_ref[h] = jnp.minimum(z2, 0.0) - soft
            drop = jnp.maximum(z2, 0.0) + soft
            if diagonal:
                drop = jnp.where(causal, drop, 0.0)
            hi = drop.astype(BF16)
            hl_ref[h, 0:TKS, :] = hi
            hl_ref[h, TKS:2 * TKS, :] = (drop - hi.astype(F32)).astype(BF16)
        for h in range(N_HEADS):
            below = jnp.dot(after2, hl_ref[h], preferred_element_type=F32)
            gone = gone_ref[h:h + 1, :]
            a = jnp.exp2(lb_ref[h] - below - gone)
            if diagonal:
                a = jnp.where(causal, a, 0.0)
            a_ref[h] = a.astype(BF16)
            first = hl_ref[h, 0:1, :].astype(F32) + hl_ref[h, TKS:TKS + 1, :].astype(F32)
            gone_ref[h:h + 1, :] = gone + below[0:1, :] + first
        for h in range(N_HEADS):
            cols = _head_cols(h)
            acc_ref[cols, :] += lax.dot_general(v_ref[rows, cols], a_ref[h], TN_DIMS,
                                                preferred_element_type=F32)

    def still_live():
        return (jnp.min(gone_ref[...]) <= SB_UNDERFLOW).astype(I32)

    per_block = TQ // TKS
    for i in range(per_block):
        step(j * per_block + (per_block - 1 - i), True)

    def walk(carry):
        ks, _ = carry
        step(ks, False)
        return ks - 1, still_live()

    lax.while_loop(lambda c: jnp.logical_and(c[0] >= 0, c[1] > 0), walk,
                   (j * per_block - 1, still_live()))
    out_ref[0] = acc_ref[...].astype(BF16)


def _sb(qkvs, batch, seq):
    nq = seq // TQ
    return pl.pallas_call(
        _sb_kernel,
        grid=(batch, nq),
        in_specs=[pl.BlockSpec((TQ, WIDTH), lambda b, j: (b * nq + j, 0)),
                  pl.BlockSpec((seq, WIDTH), lambda b, j: (b, 1)),
                  pl.BlockSpec((seq, WIDTH), lambda b, j: (b, 2))],
        out_specs=pl.BlockSpec((1, WIDTH, TQ), lambda b, j: (b, 0, j)),
        out_shape=jax.ShapeDtypeStruct((batch, WIDTH, seq), BF16),
        scratch_shapes=[pltpu.VMEM((N_HEADS, TQ), F32), pltpu.VMEM((WIDTH, TQ), F32),
                        pltpu.VMEM((N_HEADS, TKS, TQ), F32),
                        pltpu.VMEM((N_HEADS, 2 * TKS, TQ), BF16),
                        pltpu.VMEM((N_HEADS, TKS, TQ), BF16)],
        compiler_params=_params(2),
        name="sb",
    )(qkvs, qkvs, qkvs)


def _layer_norm(r, gain, bias):
    mu = jnp.mean(r, axis=-1, keepdims=True)
    c = r - mu
    var = jnp.mean(c * c, axis=-1, keepdims=True)
    return c * lax.rsqrt(var + LN_EPS) * gain + bias


def _merge_kernel(x_ref, ya_ref, ys_ref, wg_ref, bg_ref, wa_ref, wb_ref, wo_ref, g_ref, b_ref,
                  out_ref, *, alpha):
    x = x_ref[...]
    d = x.shape[1]
    gates = jnp.dot(x.astype(BF16), wg_ref[...], preferred_element_type=F32)
    gate_a = jax.nn.sigmoid(gates[:, :d] + bg_ref[0:1, :])
    gate_b = jax.nn.sigmoid(gates[:, d:] + bg_ref[1:2, :])
    branch_a = lax.dot_general(ya_ref[0], wa_ref[...], TN_DIMS, preferred_element_type=F32)
    branch_b = lax.dot_general(ys_ref[0], wb_ref[...], TN_DIMS, preferred_element_type=F32)
    merged = gate_a * branch_a + gate_b * branch_b
    mix = jnp.dot(merged.astype(BF16), wo_ref[...], preferred_element_type=F32)
    out_ref[...] = _layer_norm(alpha * x + mix, g_ref[...], b_ref[...])


def _merge(x2, ya_t, ys_t, wg, bg, wa, wb, wo, gain, bias, seq, tm, alpha):
    m, d = x2.shape
    per_seq = seq // tm
    const = lambda i: (0, 0)
    row = lambda i: (i, 0)
    y_blk = pl.BlockSpec((1, WIDTH, tm), lambda i: (i // per_seq, 0, i % per_seq))
    return pl.pallas_call(
        functools.partial(_merge_kernel, alpha=alpha),
        grid=(m // tm,),
        in_specs=[pl.BlockSpec((tm, d), row), y_blk, y_blk,
                  pl.BlockSpec(wg.shape, const), pl.BlockSpec(bg.shape, const),
                  pl.BlockSpec(wa.shape, const), pl.BlockSpec(wb.shape, const),
                  pl.BlockSpec(wo.shape, const), pl.BlockSpec(gain.shape, const),
                  pl.BlockSpec(bias.shape, const)],
        out_specs=pl.BlockSpec((tm, d), row),
        out_shape=jax.ShapeDtypeStruct((m, d), F32),
        compiler_params=_params(1),
        name="merge",
    )(x2, ya_t, ys_t, wg, bg, wa, wb, wo, gain, bias)


HALO = 8


def _gelu_tanh(x):
    return 0.5 * x * (1.0 + jnp.tanh(math.sqrt(2.0 / math.pi) * (x + 0.044715 * (x * x * x))))


def _ffn_kernel(x_ref, wu_ref, wv_ref, cw_ref, cb_ref, wo_ref, g_ref, b_ref, out_ref, u_ref,
                *, alpha, per_seq):
    tm = x_ref.shape[0]
    x = x_ref[...]
    xb = x.astype(BF16)

    @pl.when(pl.program_id(0) % per_seq == 0)
    def _():
        u_ref[0:HALO, :] = jnp.zeros((HALO, u_ref.shape[1]), F32)

    u_ref[HALO:HALO + tm, :] = jnp.dot(xb, wu_ref[...], preferred_element_type=F32)
    conv = (cw_ref[2:3, :] * u_ref[HALO:HALO + tm, :]
            + cw_ref[1:2, :] * u_ref[HALO - 1:HALO - 1 + tm, :]
            + cw_ref[0:1, :] * u_ref[HALO - 2:HALO - 2 + tm, :]
            + cb_ref[...])
    u_ref[0:HALO, :] = u_ref[tm:tm + HALO, :]
    gate = jnp.dot(xb, wv_ref[...], preferred_element_type=F32)
    hidden = (_gelu_tanh(conv) * gate).astype(BF16)
    y = jnp.dot(hidden, wo_ref[...], preferred_element_type=F32)
    out_ref[...] = _layer_norm(alpha * x + y, g_ref[...], b_ref[...])


def _ffn(x1, wu, wv, cw, cb, wo, gain, bias, seq, tm, alpha):
    m, d = x1.shape
    f = wu.shape[1]
    const = lambda i: (0, 0)
    row = lambda i: (i, 0)
    once = dict(pipeline_mode=pl.Buffered(1))
    return pl.pallas_call(
        functools.partial(_ffn_kernel, alpha=alpha, per_seq=seq // tm),
        grid=(m // tm,),
        in_specs=[pl.BlockSpec((tm, d), row),
                  pl.BlockSpec(wu.shape, const, **once), pl.BlockSpec(wv.shape, const, **once),
                  pl.BlockSpec(cw.shape, const), pl.BlockSpec(cb.shape, const),
                  pl.BlockSpec(wo.shape, const, **once),
                  pl.BlockSpec(gain.shape, const), pl.BlockSpec(bias.shape, const)],
        out_specs=pl.BlockSpec((tm, d), row),
        out_shape=jax.ShapeDtypeStruct((m, d), F32),
        scratch_shapes=[pltpu.VMEM((HALO + tm, f), F32)],
        compiler_params=_params(1),
        name="ffn",
    )(x1, wu, wv, cw, cb, wo, gain, bias)


def kernel(x, rel_bias, w_in, b_gates, w_branch_dsa, w_branch_sb, w_out, ln1_g, ln1_b,
           w_ffn_in, conv_w, conv_b, w_ffn_out, ln2_g, ln2_b):
    batch, seq, d = x.shape
    depth = w_in.shape[0]
    d_ff = conv_w.shape[-1]
    assert seq % TQ == 0 and TQ == TK and TQ % TKS == 0 and TQ % CHUNK == 0
    assert w_in.shape[-1] == 7 * WIDTH + IDX_DIM + N_HEADS + 2 * d
    alpha = (2.0 * depth) ** 0.25
    n_sel = min(TOPK_MAX, seq // 4)
    assert n_sel <= TK
    tm = min(512, seq)
    o_qi = 3 * WIDTH
    o_ki = o_qi + WIDTH
    o_wi = o_ki + IDX_DIM
    o_sb = o_wi + N_HEADS
    o_g = o_sb + 3 * WIDTH
    assert HEAD_DIM ** -0.5 == 0.125 and IDX_DIM ** -0.5 == 0.125
    q_scale = jnp.ones((w_in.shape[-1],), F32)
    q_scale = q_scale.at[0:WIDTH].set(HEAD_DIM ** -0.5)
    q_scale = q_scale.at[o_qi:o_ki].set(IDX_DIM ** -0.5)
    q_scale = q_scale.at[o_sb:o_sb + WIDTH].set(HEAD_DIM ** -0.5)

    near_bias = _relbias_tiles(rel_bias)
    x2 = x.reshape(batch * seq, d)
    for layer in range(depth):
        w = (w_in[layer] * q_scale).astype(BF16)
        qkva, q_idx, k_idx, w_t, qkvs = _proj(
            x2, w[:, :o_qi], w[:, o_qi:o_ki], w[:, o_ki:o_wi], w[:, o_wi:o_sb].T, w[:, o_sb:o_g],
            batch, seq, tm)
        ya_t = _dsa(rel_bias, qkva, q_idx, k_idx, w_t, near_bias, batch, seq, n_sel)
        ys_t = _sb(qkvs, batch, seq)
        x2 = _merge(x2, ya_t, ys_t, w[:, o_g:], b_gates[layer],
                    w_branch_dsa[layer].astype(BF16), w_branch_sb[layer].astype(BF16),
                    w_out[layer].astype(BF16), ln1_g[layer][None], ln1_b[layer][None],
                    seq, tm, alpha)
        w_ffn = w_ffn_in[layer].astype(BF16)
        x2 = _ffn(x2, w_ffn[:, :d_ff], w_ffn[:, d_ff:], conv_w[layer], conv_b[layer][None],
                  w_ffn_out[layer].astype(BF16), ln2_g[layer][None], ln2_b[layer][None],
                  seq, tm, alpha)
    return x2.reshape(batch, seq, d)
```
